```python
import math
import jax, jax.numpy as jnp
from jax import lax
import numpy as np

D_MODEL = 2048
BATCH = 2
SEQ = 8192
DEPTH = 4

N_MEM = 256
D_MIX = 2 * D_MODEL
D_SSD = D_MIX // 2
D_SC = D_MIX - D_SSD
SSD_HEADDIM = 64
SSD_HEADS = D_SSD // SSD_HEADDIM
SSD_GROUPS = 4
SSD_STATE = 128
SSD_CONV = 4
SSD_CHUNK = 256
D_XBC = D_SSD + 2 * SSD_GROUPS * SSD_STATE
DT_MIN = 1e-3
DT_MAX = 1e-1
SC_CONV = 3
SC_GROUPS = 16
XA_HEADS = 4
XA_HEADDIM = 128
D_XA = XA_HEADS * XA_HEADDIM
D_FF = ((8 * D_MODEL + 3 * 256 - 1) // (3 * 256)) * 256
NORM_EPS = 1e-5
D_IN_PROJ = D_SSD + D_XBC + SSD_HEADS + 3 * D_SC
IN_SPLITS = [D_SSD,
             D_SSD + D_XBC,
             D_SSD + D_XBC + SSD_HEADS,
             D_SSD + D_XBC + SSD_HEADS + D_SC,
             D_SSD + D_XBC + SSD_HEADS + 2 * D_SC]

kernel_name = "hybrid_ssd_shortconv_memxattn_trunk"


def rmsnorm(x, g):
    xf = x.astype(jnp.float32)
    xf = xf * lax.rsqrt(jnp.mean(xf * xf, axis=-1, keepdims=True) + NORM_EPS)
    return (xf * g.astype(jnp.float32)).astype(x.dtype)


def grouped_rmsnorm(x, g, n_groups):
    shp = x.shape
    xf = x.astype(jnp.float32).reshape(shp[:-1] + (n_groups, shp[-1] // n_groups))
    xf = xf * lax.rsqrt(jnp.mean(xf * xf, axis=-1, keepdims=True) + NORM_EPS)
    return (xf.reshape(shp) * g.astype(jnp.float32)).astype(x.dtype)


def causal_dwconv(x, w):
    k, c = w.shape
    return lax.conv_general_dilated(
        x, w[:, None, :].astype(x.dtype), window_strides=(1,), padding=[(k - 1, 0)],
        dimension_numbers=("NWC", "WIO", "NWC"), feature_group_count=c)


def ssd_chunked(xh, dt, a, bm, cm):
    f32 = jnp.float32
    bsz, seqlen, nh, hd = xh.shape
    g, n = bm.shape[2], bm.shape[3]
    r = nh // g
    pad = (-seqlen) % SSD_CHUNK
    xh, dt, bm, cm = (t.astype(f32) for t in (xh, dt, bm, cm))
    if pad:
        xh = jnp.pad(xh, ((0, 0), (0, pad), (0, 0), (0, 0)))
        dt = jnp.pad(dt, ((0, 0), (0, pad), (0, 0)))
        bm = jnp.pad(bm, ((0, 0), (0, pad), (0, 0), (0, 0)))
        cm = jnp.pad(cm, ((0, 0), (0, pad), (0, 0), (0, 0)))
    nc = (seqlen + pad) // SSD_CHUNK
    L = SSD_CHUNK
    x = (xh * dt[..., None]).reshape(bsz, nc, L, g, r, hd)
    a_dt = jnp.moveaxis((dt * a.astype(f32)).reshape(bsz, nc, L, g, r), 2, -1)
    a_cs = jnp.cumsum(a_dt, axis=-1)
    bc = bm.reshape(bsz, nc, L, g, n)
    cc = cm.reshape(bsz, nc, L, g, n)
    causal = jnp.tril(jnp.ones((L, L), dtype=bool))
    decay = jnp.exp(jnp.where(causal, a_cs[..., :, None] - a_cs[..., None, :], -jnp.inf))
    cb = jnp.einsum("bclgn,bcsgn->bcgls", cc, bc)
    scores = cb[:, :, :, None] * decay
    y_diag = jnp.einsum("bcgrls,bcsgrp->bclgrp", scores, x)
    decay_states = jnp.exp(a_cs[..., -1:] - a_cs)
    states = jnp.einsum("bclgn,bcgrl,bclgrp->bcgrpn", bc, decay_states, x)
    chunk_decay = jnp.exp(a_cs[..., -1])

    def step(h, inp):
        s_c, d_c = inp
        return h * d_c[..., None, None] + s_c, h

    h0 = jnp.zeros_like(states[:, 0])
    _, prev = lax.scan(step, h0, (jnp.moveaxis(states, 1, 0), jnp.moveaxis(chunk_decay, 1, 0)))
    prev = jnp.moveaxis(prev, 0, 1)
    y_off = jnp.einsum("bclgn,bcgrpn,bcgrl->bclgrp", cc, prev, jnp.exp(a_cs))
    y = (y_diag + y_off).reshape(bsz, nc * L, nh, hd)
    return y[:, :seqlen]


def hybrid_mixer(h, w_in, ssd_conv_w, ssd_conv_b, dt_bias, a_log, d_skip, ssd_norm,
                 sc_conv_w, sc_norm, w_out):
    bsz, seqlen, _ = h.shape
    proj = h @ w_in
    z, xbc, dt_raw, sc_u, sc_b, sc_c = jnp.split(proj, IN_SPLITS, axis=-1)
    xbc = jax.nn.silu(causal_dwconv(xbc, ssd_conv_w) + ssd_conv_b)
    xs, bm, cm = jnp.split(xbc, [D_SSD, D_SSD + SSD_GROUPS * SSD_STATE], axis=-1)
    xh = xs.reshape(bsz, seqlen, SSD_HEADS, SSD_HEADDIM)
    dt = jax.nn.softplus(dt_raw.astype(jnp.float32) + dt_bias.astype(jnp.float32))
    a = -jnp.exp(a_log.astype(jnp.float32))
    y = ssd_chunked(xh, dt, a,
                    bm.reshape(bsz, seqlen, SSD_GROUPS, SSD_STATE),
                    cm.reshape(bsz, seqlen, SSD_GROUPS, SSD_STATE))
    y = y + xh.astype(jnp.float32) * d_skip.astype(jnp.float32)[:, None]
    y = y.reshape(bsz, seqlen, D_SSD).astype(h.dtype)
    y_ssd = grouped_rmsnorm(y * jax.nn.silu(z), ssd_norm, SSD_GROUPS)
    v = sc_b * causal_dwconv(sc_c * sc_u, sc_conv_w)
    y_sc = grouped_rmsnorm(v, sc_norm, SC_GROUPS)
    return jnp.concatenate([y_ssd, y_sc], axis=-1) @ w_out


def memory_cross_attention(h, mem_n, w_q, w_k, w_v, w_o):
    bsz, seqlen, _ = h.shape
    n_mem = mem_n.shape[1]
    q = (h @ w_q).reshape(bsz, seqlen, XA_HEADS, XA_HEADDIM)
    k = (mem_n @ w_k).reshape(bsz, n_mem, XA_HEADS, XA_HEADDIM)
    v = (mem_n @ w_v).reshape(bsz, n_mem, XA_HEADS, XA_HEADDIM)
    scores = jnp.einsum("bshd,bmhd->bhsm", q, k).astype(jnp.float32) * (XA_HEADDIM ** -0.5)
    p = jax.nn.softmax(scores, axis=-1).astype(v.dtype)
    o = jnp.einsum("bhsm,bmhd->bshd", p, v).reshape(bsz, seqlen, D_XA)
    return o @ w_o


def swiglu(h, w_gate, w_up, w_down):
    return (jax.nn.silu(h @ w_gate) * (h @ w_up)) @ w_down


def setup_inputs(seed: int = 0) -> dict:
    key = jax.random.key(seed)
    ks = jax.random.split(key, 24)
    f32 = jnp.float32

    def nrm(k, shape, scale):
        return jax.random.normal(k, shape, f32) * scale

    def gain(k, shape):
        return 1.0 + 0.02 * jax.random.normal(k, shape, f32)

    x = nrm(ks[0], (BATCH, SEQ, D_MODEL), 1.0)
    mem = nrm(ks[1], (BATCH, N_MEM, D_MODEL), 1.0)
    norm_mix = gain(ks[2], (DEPTH, D_MODEL))
    w_in = nrm(ks[3], (DEPTH, D_MODEL, D_IN_PROJ), D_MODEL ** -0.5)
    ssd_conv_w = nrm(ks[4], (DEPTH, SSD_CONV, D_XBC), SSD_CONV ** -0.5)
    ssd_conv_b = nrm(ks[5], (DEPTH, D_XBC), 0.02)
    dt0 = jnp.exp(jax.random.uniform(ks[6], (DEPTH, SSD_HEADS), f32, math.log(DT_MIN), math.log(DT_MAX)))
    dt_bias = dt0 + jnp.log(-jnp.expm1(-dt0))
    a_log = jnp.log(jax.random.uniform(ks[7], (DEPTH, SSD_HEADS), f32, 1.0, 16.0))
    d_skip = gain(ks[8], (DEPTH, SSD_HEADS))
    ssd_norm = gain(ks[9], (DEPTH, D_SSD))
    sc_conv_w = nrm(ks[10], (DEPTH, SC_CONV, D_SC), SC_CONV ** -0.5)
    sc_norm = gain(ks[11], (DEPTH, D_SC))
    w_out = nrm(ks[12], (DEPTH, D_MIX, D_MODEL), D_MIX ** -0.5)
    mem_norm = gain(ks[13], (D_MODEL,))
    norm_xa = gain(ks[14], (DEPTH, D_MODEL))
    w_q = nrm(ks[15], (DEPTH, D_MODEL, D_XA), D_MODEL ** -0.5)
    w_k = nrm(ks[16], (DEPTH, D_MODEL, D_XA), D_MODEL ** -0.5)
    w_v = nrm(ks[17], (DEPTH, D_MODEL, D_XA), D_MODEL ** -0.5)
    w_o = nrm(ks[18], (DEPTH, D_XA, D_MODEL), D_XA ** -0.5)
    norm_ffn = gain(ks[19], (DEPTH, D_MODEL))
    w_gate = nrm(ks[20], (DEPTH, D_MODEL, D_FF), D_MODEL ** -0.5)
    w_up = nrm(ks[21], (DEPTH, D_MODEL, D_FF), D_MODEL ** -0.5)
    w_down = nrm(ks[22], (DEPTH, D_FF, D_MODEL), D_FF ** -0.5)
    norm_final = gain(ks[23], (D_MODEL,))
    return {"x": x, "mem": mem, "norm_mix": norm_mix, "w_in": w_in,
            "ssd_conv_w": ssd_conv_w, "ssd_conv_b": ssd_conv_b, "dt_bias": dt_bias,
            "a_log": a_log, "d_skip": d_skip, "ssd_norm": ssd_norm,
            "sc_conv_w": sc_conv_w, "sc_norm": sc_norm, "w_out": w_out,
            "mem_norm": mem_norm, "norm_xa": norm_xa, "w_q": w_q, "w_k": w_k,
            "w_v": w_v, "w_o": w_o, "norm_ffn": norm_ffn, "w_gate": w_gate,
            "w_up": w_up, "w_down": w_down, "norm_final": norm_final}


def reference(x, mem, norm_mix, w_in, ssd_conv_w, ssd_conv_b, dt_bias, a_log, d_skip,
              ssd_norm, sc_conv_w, sc_norm, w_out, mem_norm, norm_xa, w_q, w_k, w_v, w_o,
              norm_ffn, w_gate, w_up, w_down, norm_final):
    mem_n = rmsnorm(mem, mem_norm)
    h = x
    for i in range(DEPTH):
        h = h + hybrid_mixer(rmsnorm(h, norm_mix[i]), w_in[i], ssd_conv_w[i], ssd_conv_b[i],
                             dt_bias[i], a_log[i], d_skip[i], ssd_norm[i],
                             sc_conv_w[i], sc_norm[i], w_out[i])
        h = h + memory_cross_attention(rmsnorm(h, norm_xa[i]), mem_n, w_q[i], w_k[i], w_v[i], w_o[i])
        h = h + swiglu(rmsnorm(h, norm_ffn[i]), w_gate[i], w_up[i], w_down[i])
    return rmsnorm(h, norm_final)
```

```python
import functools

import jax
import jax.numpy as jnp
from jax import lax
from jax.experimental import pallas as pl
from jax.experimental.pallas import tpu as pltpu

F32 = jnp.float32
BF16 = jnp.bfloat16
HIGHEST = lax.Precision.HIGHEST

NORM_EPS = 1e-5
SSD_HEADDIM = 64
SSD_GROUPS = 4
SSD_STATE = 128
SSD_CONV = 4
SSD_CHUNK = 256
SC_CONV = 3
SC_GROUPS = 16
XA_HEADS = 4
XA_HEADDIM = 128

V7X_LANES = 128
V7X_SUBLANES = 8
V7X_VMEM_BYTES = 64 * 1024 * 1024
MIB = 1024 * 1024

MIXER_COLS = 512


def _vmem_limit(estimate_bytes):
    return int(min(estimate_bytes * 1.25 + 8 * MIB, V7X_VMEM_BYTES - 6 * MIB))


def _rms(x, g):
    ms = jnp.mean(x * x, axis=-1, keepdims=True)
    return x * lax.rsqrt(ms + NORM_EPS) * g


def _dot(a, b):
    return jnp.dot(a, b, preferred_element_type=F32)


def _dot_nt(a, b):
    return lax.dot_general(a, b, (((1,), (1,)), ((), ())), preferred_element_type=F32)


def _dot_exact(a, b):
    return jnp.dot(a, b, precision=HIGHEST, preferred_element_type=F32)


def _in_proj_kernel(h_ref, g_ref, w_ref, wdt_ref, dtb_ref, proj_ref, dt_ref, xn_ref):
    @pl.when(pl.program_id(1) == 0)
    def _():
        xn = _rms(h_ref[...], g_ref[...]).astype(BF16)
        xn_ref[...] = xn
        dt_ref[...] = jax.nn.softplus(_dot(xn, wdt_ref[...]) + dtb_ref[...])

    proj_ref[...] = _dot(xn_ref[...], w_ref[...]).astype(BF16)


def _in_proj(h, g, w_all, wdt_all, dtb_all, layer, *, tm=1024, tn=1024):
    t, d = h.shape
    n = w_all.shape[-1]
    est = 2 * tm * d * 4 + 2 * d * tn * 2 + 2 * tm * tn * 2 + tm * d * 2 + tm * tn * 4 + 2 * d * V7X_LANES * 2
    return pl.pallas_call(
        _in_proj_kernel,
        grid=(t // tm, n // tn),
        in_specs=[
            pl.BlockSpec((tm, d), lambda i, j: (i, 0)),
            pl.BlockSpec((None, 1, d), lambda i, j: (layer, 0, 0)),
            pl.BlockSpec((None, d, tn), lambda i, j: (layer, 0, j)),
            pl.BlockSpec((None, d, V7X_LANES), lambda i, j: (layer, 0, 0)),
            pl.BlockSpec((None, 1, V7X_LANES), lambda i, j: (layer, 0, 0)),
        ],
        out_specs=[
            pl.BlockSpec((tm, tn), lambda i, j: (i, j)),
            pl.BlockSpec((tm, V7X_LANES), lambda i, j: (i, 0)),
        ],
        out_shape=[jax.ShapeDtypeStruct((t, n), BF16), jax.ShapeDtypeStruct((t, V7X_LANES), F32)],
        scratch_shapes=[pltpu.VMEM((tm, d), BF16)],
        compiler_params=pltpu.CompilerParams(
            dimension_semantics=("parallel", "arbitrary"), vmem_limit_bytes=_vmem_limit(est)),
        name="in_proj",
    )(h, g, w_all, wdt_all, dtb_all)


def _mixer_kernel(proj_ref, dt_ref, cw_ref, cb_ref, alog_ref, dsk_ref, gssd_ref, scw_ref, gsc_ref, e_ref,
                  y_ref, xpad, xact, scpad, ybuf, state, *, d_ssd, d_sc):
    L = SSD_CHUNK
    pad = V7X_SUBLANES
    d_bc = SSD_GROUPS * SSD_STATE
    d_xbc = d_ssd + 2 * d_bc
    o_xbc = d_ssd
    o_u = o_xbc + d_xbc
    o_gb = o_u + d_sc
    o_gc = o_gb + d_sc
    heads_per_group = d_ssd // SSD_HEADDIM // SSD_GROUPS
    gcols = heads_per_group * SSD_HEADDIM
    pair = 2 * SSD_HEADDIM
    cw = MIXER_COLS

    @pl.when(pl.program_id(1) == 0)
    def _():
        xpad[0:pad, :] = jnp.zeros((pad, d_xbc), F32)
        scpad[0:pad, :] = jnp.zeros((pad, d_sc), F32)
        state[...] = jnp.zeros_like(state)

    for j0 in range(0, d_xbc, cw):
        cs = slice(j0, j0 + cw)
        xpad[pad:pad + L, cs] = proj_ref[0, :, o_xbc + j0:o_xbc + j0 + cw].astype(F32)
        acc = cb_ref[:, cs]
        for k in range(SSD_CONV):
            r0 = pad - (SSD_CONV - 1) + k
            acc = acc + cw_ref[k:k + 1, cs] * xpad[r0:r0 + L, cs]
        xact[:, cs] = jax.nn.silu(acc)
        xpad[0:pad, cs] = xpad[L:L + pad, cs]

    dt = dt_ref[0]
    adt = dt * (-jnp.exp(alog_ref[...]))
    row = lax.broadcasted_iota(jnp.int32, (L, L), 0)
    col = lax.broadcasted_iota(jnp.int32, (L, L), 1)
    tril = row >= col
    acs = _dot_exact(tril.astype(F32), adt)
    acs_t = acs.T
    acs_last = acs[L - 1:L, :]
    e_acs = jnp.exp(acs)
    e_rest = jnp.exp(acs_last - acs)
    e_chunk = jnp.broadcast_to(jnp.exp(acs_last), (pad, V7X_LANES))
    lane = lax.broadcasted_iota(jnp.int32, (L, pair), 1)
    first_head = lane < SSD_HEADDIM

    for g in range(SSD_GROUPS):
        xs = slice(g * gcols, (g + 1) * gcols)
        bg = xact[:, d_ssd + g * SSD_STATE:d_ssd + (g + 1) * SSD_STATE]
        cg16 = xact[:, d_ssd + d_bc + g * SSD_STATE:d_ssd + d_bc + (g + 1) * SSD_STATE].astype(BF16)
        cb = _dot_nt(cg16, bg.astype(BF16))
        eg = e_ref[:, xs]
        dt_x = _dot_exact(dt, eg)
        xg = xact[:, xs]
        xdt = xg * dt_x
        y_off = _dot(cg16, state[g].astype(BF16)) * _dot_exact(e_acs, eg)
        for p in range(gcols // pair):
            ps = slice(p * pair, (p + 1) * pair)
            xp = xdt[:, ps]
            yd = None
            for s in range(2):
                hd = g * heads_per_group + 2 * p + s
                seg = acs[:, hd:hd + 1] - acs_t[hd:hd + 1, :]
                decay = jnp.exp(jnp.where(tril, seg, -jnp.inf))
                scores = (cb * decay).astype(BF16)
                keep = first_head if s == 0 else jnp.logical_not(first_head)
                part = _dot(scores, jnp.where(keep, xp, 0.0).astype(BF16))
                yd = part if yd is None else yd + part
            ys = slice(g * gcols + p * pair, g * gcols + (p + 1) * pair)
            y = yd + y_off[:, ps] + xg[:, ps] * dsk_ref[:, ys]
            ybuf[:, ys] = y * jax.nn.silu(proj_ref[0, :, ys].astype(F32))
        xrest = (xdt * _dot_exact(e_rest, eg)).astype(BF16)
        new = _dot(bg.T.astype(BF16), xrest)
        state[g] = state[g] * _dot_exact(e_chunk, eg)[0:1, :] + new
        yg = ybuf[:, xs]
        y_ref[0, :, xs] = _rms(yg, gssd_ref[:, xs]).astype(BF16)

    gw = d_sc // SC_GROUPS
    for j0 in range(0, d_sc, cw):
        cs = slice(j0, j0 + cw)
        u = proj_ref[0, :, o_u + j0:o_u + j0 + cw].astype(F32)
        gate_b = proj_ref[0, :, o_gb + j0:o_gb + j0 + cw].astype(F32)
        gate_c = proj_ref[0, :, o_gc + j0:o_gc + j0 + cw].astype(F32)
        scpad[pad:pad + L, cs] = gate_c * u
        acc = None
        for k in range(SC_CONV):
            r0 = pad - (SC_CONV - 1) + k
            term = scw_ref[k:k + 1, cs] * scpad[r0:r0 + L, cs]
            acc = term if acc is None else acc + term
        v = gate_b * acc
        scpad[0:pad, cs] = scpad[L:L + pad, cs]
        for q0 in range(0, cw, gw):
            y_ref[0, :, d_ssd + j0 + q0:d_ssd + j0 + q0 + gw] = _rms(
                v[:, q0:q0 + gw], gsc_ref[:, j0 + q0:j0 + q0 + gw]).astype(BF16)


def _mixer(proj, dt, cw, cb, alog, dsk, gssd, scw, gsc, e, layer, *, d_ssd, d_sc):
    b, s, n = proj.shape
    L = SSD_CHUNK
    d_xbc = d_ssd + 2 * SSD_GROUPS * SSD_STATE
    gcols = d_ssd // SSD_GROUPS
    assert s % L == 0 and gcols % (2 * SSD_HEADDIM) == 0 and (d_sc // SC_GROUPS) % V7X_LANES == 0
    assert d_xbc % MIXER_COLS == 0 and d_sc % MIXER_COLS == 0 and MIXER_COLS % (d_sc // SC_GROUPS) == 0
    pad = V7X_SUBLANES
    est = (2 * L * n * 2 + 2 * L * (d_ssd + d_sc) * 2 + 2 * V7X_LANES * d_ssd * 4
           + (L + pad) * (d_xbc + d_sc) * 4 + L * d_xbc * 4 + L * d_ssd * 4 + SSD_STATE * d_ssd * 4
           + 6 * L * gcols * 4)
    lay = lambda shape: pl.BlockSpec((None,) + shape, lambda bi, ci: (layer, 0, 0))
    return pl.pallas_call(
        functools.partial(_mixer_kernel, d_ssd=d_ssd, d_sc=d_sc),
        grid=(b, s // L),
        in_specs=[
            pl.BlockSpec((1, L, n), lambda bi, ci: (bi, ci, 0)),
            pl.BlockSpec((1, L, V7X_LANES), lambda bi, ci: (bi, ci, 0)),
            lay((SSD_CONV, d_xbc)), lay((1, d_xbc)), lay((1, V7X_LANES)), lay((1, d_ssd)), lay((1, d_ssd)),
            lay((SC_CONV, d_sc)), lay((1, d_sc)),
            pl.BlockSpec((V7X_LANES, d_ssd), lambda bi, ci: (0, 0)),
        ],
        out_specs=pl.BlockSpec((1, L, d_ssd + d_sc), lambda bi, ci: (bi, ci, 0)),
        out_shape=jax.ShapeDtypeStruct((b, s, d_ssd + d_sc), BF16),
        scratch_shapes=[
            pltpu.VMEM((L + pad, d_xbc), F32),
            pltpu.VMEM((L, d_xbc), F32),
            pltpu.VMEM((L + pad, d_sc), F32),
            pltpu.VMEM((L, d_ssd), F32),
            pltpu.VMEM((SSD_GROUPS, SSD_STATE, gcols), F32),
        ],
        compiler_params=pltpu.CompilerParams(
            dimension_semantics=("parallel", "arbitrary"), vmem_limit_bytes=_vmem_limit(est)),
        name="mixer",
    )(proj, dt, cw, cb, alog, dsk, gssd, scw, gsc, e)


def _out_proj_kernel(y_ref, w_ref, h_ref, o_ref):
    o_ref[...] = h_ref[...] + _dot(y_ref[...], w_ref[...])


def _out_proj(y, w_all, h, layer, *, tm=512):
    t, k = y.shape
    d = h.shape[-1]
    est = 2 * tm * k * 2 + k * d * 2 + 4 * tm * d * 4 + tm * d * 4
    return pl.pallas_call(
        _out_proj_kernel,
        grid=(t // tm,),
        in_specs=[
            pl.BlockSpec((tm, k), lambda i: (i, 0)),
            pl.BlockSpec((None, k, d), lambda i: (layer, 0, 0), pipeline_mode=pl.Buffered(1)),
            pl.BlockSpec((tm, d), lambda i: (i, 0)),
        ],
        out_specs=pl.BlockSpec((tm, d), lambda i: (i, 0)),
        out_shape=jax.ShapeDtypeStruct((t, d), F32),
        compiler_params=pltpu.CompilerParams(
            dimension_semantics=("parallel",), vmem_limit_bytes=_vmem_limit(est)),
        name="out_proj",
    )(y, w_all, h)


def _xattn_kernel(h_ref, g_ref, wq_ref, k_ref, v_ref, wo_ref, o_ref):
    h = h_ref[...]
    xn = _rms(h, g_ref[...]).astype(BF16)
    q = _dot(xn, wq_ref[...]).astype(BF16)
    scale = XA_HEADDIM ** -0.5
    heads = []
    for hd in range(XA_HEADS):
        cs = slice(hd * XA_HEADDIM, (hd + 1) * XA_HEADDIM)
        s = _dot_nt(q[:, cs], k_ref[:, cs]) * scale
        p = jnp.exp(s - jnp.max(s, axis=-1, keepdims=True))
        p = p / jnp.sum(p, axis=-1, keepdims=True)
        heads.append(_dot(p.astype(BF16), v_ref[:, cs]).astype(BF16))
    o_ref[...] = h + _dot(jnp.concatenate(heads, axis=-1), wo_ref[...])


def _xattn(h, g, wq_all, kv, wo_all, layer, depth, *, n_mem, seq, tm=512):
    t, d = h.shape
    d_xa = XA_HEADS * XA_HEADDIM
    assert seq % tm == 0
    per_batch = seq // tm
    est = 4 * tm * d * 4 + 2 * 2 * d * d_xa * 2 + 4 * n_mem * d_xa * 2 + tm * d * 6 + 4 * tm * n_mem * 4
    return pl.pallas_call(
        _xattn_kernel,
        grid=(t // tm,),
        in_specs=[
            pl.BlockSpec((tm, d), lambda i: (i, 0)),
            pl.BlockSpec((None, 1, d), lambda i: (layer, 0, 0)),
            pl.BlockSpec((None, d, d_xa), lambda i: (layer, 0, 0)),
            pl.BlockSpec((n_mem, d_xa), lambda i: (i // per_batch, layer)),
            pl.BlockSpec((n_mem, d_xa), lambda i: (i // per_batch, depth + layer)),
            pl.BlockSpec((None, d_xa, d), lambda i: (layer, 0, 0)),
        ],
        out_specs=pl.BlockSpec((tm, d), lambda i: (i, 0)),
        out_shape=jax.ShapeDtypeStruct((t, d), F32),
        compiler_params=pltpu.CompilerParams(
            dimension_semantics=("parallel",), vmem_limit_bytes=_vmem_limit(est)),
        name="xattn",
    )(h, g, wq_all, kv, kv, wo_all)


def _kv_kernel(mem_ref, g_ref, w_ref, o_ref):
    o_ref[...] = _dot(_rms(mem_ref[...], g_ref[...]).astype(BF16), w_ref[...]).astype(BF16)


def _kv_proj(mem, g, wkv, *, tn=512):
    m, d = mem.shape
    n = wkv.shape[-1]
    est = 2 * m * d * 4 + 2 * d * tn * 2 + 2 * m * tn * 2 + m * d * 6
    return pl.pallas_call(
        _kv_kernel,
        grid=(n // tn,),
        in_specs=[
            pl.BlockSpec((m, d), lambda j: (0, 0)),
            pl.BlockSpec((1, d), lambda j: (0, 0)),
            pl.BlockSpec((d, tn), lambda j: (0, j)),
        ],
        out_specs=pl.BlockSpec((m, tn), lambda j: (0, j)),
        out_shape=jax.ShapeDtypeStruct((m, n), BF16),
        compiler_params=pltpu.CompilerParams(
            dimension_semantics=("parallel",), vmem_limit_bytes=_vmem_limit(est)),
        name="kv_proj",
    )(mem, g, wkv)


def _ffn_kernel(h_ref, g_ref, wg_ref, wu_ref, wd_ref, o_ref, xn_ref):
    @pl.when(pl.program_id(1) == 0)
    def _():
        h = h_ref[...]
        xn_ref[...] = _rms(h, g_ref[...]).astype(BF16)
        o_ref[...] = h

    xn = xn_ref[...]
    act = (jax.nn.silu(_dot(xn, wg_ref[...])) * _dot(xn, wu_ref[...])).astype(BF16)
    o_ref[...] += _dot(act, wd_ref[...])


def _ffn(h, g, wg_all, wu_all, wd_all, layer, *, tm=512, tf=512):
    t, d = h.shape
    f = wg_all.shape[-1]
    est = 4 * tm * d * 4 + tm * d * 2 + 3 * 2 * d * tf * 2 + 3 * tm * tf * 4 + tm * d * 4
    return pl.pallas_call(
        _ffn_kernel,
        grid=(t // tm, f // tf),
        in_specs=[
            pl.BlockSpec((tm, d), lambda i, j: (i, 0)),
            pl.BlockSpec((None, 1, d), lambda i, j: (layer, 0, 0)),
            pl.BlockSpec((None, d, tf), lambda i, j: (layer, 0, j)),
            pl.BlockSpec((None, d, tf), lambda i, j: (layer, 0, j)),
            pl.BlockSpec((None, tf, d), lambda i, j: (layer, j, 0)),
        ],
        out_specs=pl.BlockSpec((tm, d), lambda i, j: (i, 0)),
        out_shape=jax.ShapeDtypeStruct((t, d), F32),
        scratch_shapes=[pltpu.VMEM((tm, d), BF16)],
        compiler_params=pltpu.CompilerParams(
            dimension_semantics=("parallel", "arbitrary"), vmem_limit_bytes=_vmem_limit(est)),
        name="ffn",
    )(h, g, wg_all, wu_all, wd_all)


def _norm_kernel(h_ref, g_ref, o_ref):
    o_ref[...] = _rms(h_ref[...], g_ref[...])


def _final_norm(h, g, *, tm=1024):
    t, d = h.shape
    return pl.pallas_call(
        _norm_kernel,
        grid=(t // tm,),
        in_specs=[pl.BlockSpec((tm, d), lambda i: (i, 0)), pl.BlockSpec((1, d), lambda i: (0, 0))],
        out_specs=pl.BlockSpec((tm, d), lambda i: (i, 0)),
        out_shape=jax.ShapeDtypeStruct((t, d), F32),
        compiler_params=pltpu.CompilerParams(
            dimension_semantics=("parallel",), vmem_limit_bytes=_vmem_limit(4 * tm * d * 4 + tm * d * 4)),
        name="final_norm",
    )(h, g)


def kernel(x, mem, norm_mix, w_in, ssd_conv_w, ssd_conv_b, dt_bias, a_log, d_skip, ssd_norm, sc_conv_w, sc_norm,
           w_out, mem_norm, norm_xa, w_q, w_k, w_v, w_o, norm_ffn, w_gate, w_up, w_down, norm_final):
    b, s, d = x.shape
    depth = w_in.shape[0]
    n_mem = mem.shape[1]
    d_ssd = ssd_norm.shape[-1]
    d_sc = sc_norm.shape[-1]
    n_heads = dt_bias.shape[-1]
    d_xbc = ssd_conv_w.shape[-1]
    assert n_heads * SSD_HEADDIM == d_ssd and n_heads <= V7X_LANES

    c_dt = d_ssd + d_xbc
    w_main = jnp.concatenate([w_in[:, :, :c_dt], w_in[:, :, c_dt + n_heads:]], axis=-1).astype(BF16)
    lane_pad = ((0, 0), (0, 0), (0, V7X_LANES - n_heads))
    w_dt = jnp.pad(w_in[:, :, c_dt:c_dt + n_heads], lane_pad).astype(BF16)
    dtb = jnp.pad(dt_bias[:, None, :], lane_pad)
    alog = jnp.pad(a_log[:, None, :], lane_pad)
    dsk = jnp.repeat(d_skip, SSD_HEADDIM, axis=-1)[:, None, :]
    head_of_col = jnp.arange(d_ssd, dtype=jnp.int32)[None, :] // SSD_HEADDIM
    expand = (head_of_col == jnp.arange(V7X_LANES, dtype=jnp.int32)[:, None]).astype(F32)
    w_out16, w_q16, w_o16 = w_out.astype(BF16), w_q.astype(BF16), w_o.astype(BF16)
    w_gate16, w_up16, w_down16 = w_gate.astype(BF16), w_up.astype(BF16), w_down.astype(BF16)
    w_kv = jnp.concatenate([w_k[i] for i in range(depth)] + [w_v[i] for i in range(depth)], axis=-1).astype(BF16)
    row = lambda p: p[:, None, :]

    kv = _kv_proj(mem.reshape(b * n_mem, d), mem_norm[None, :], w_kv)
    h = x.reshape(b * s, d)
    for i in range(depth):
        proj, dt = _in_proj(h, row(norm_mix), w_main, w_dt, dtb, i)
        y = _mixer(proj.reshape(b, s, -1), dt.reshape(b, s, V7X_LANES), ssd_conv_w, row(ssd_conv_b), alog, dsk,
                   row(ssd_norm), sc_conv_w, row(sc_norm), expand, i, d_ssd=d_ssd, d_sc=d_sc)
        h = _out_proj(y.reshape(b * s, d_ssd + d_sc), w_out16, h, i)
        h = _xattn(h, row(norm_xa), w_q16, kv, w_o16, i, depth, n_mem=n_mem, seq=s)
        h = _ffn(h, row(norm_ffn), w_gate16, w_up16, w_down16, i)
    return _final_norm(h, norm_final[None, :]).reshape(b, s, d)
```

```python
import functools
import math

import jax
import jax.numpy as jnp
from jax import lax
from jax.experimental import pallas as pl
from jax.experimental.pallas import tpu as pltpu

F32 = jnp.float32
BF16 = jnp.bfloat16
LOG2_E = math.log2(math.e)

NORM_EPS = 1e-5
SSD_HEADDIM = 64
SSD_GROUPS = 4
SSD_STATE = 128
SSD_CONV = 4
SSD_CHUNK = 256
SC_CONV = 3
SC_GROUPS = 16
XA_HEADS = 4
XA_HEADDIM = 128

V7X_LANES = 128
V7X_SUBLANES = 8
V7X_VMEM_BYTES = 64 * 1024 * 1024
MIB = 1024 * 1024

MIXER_COLS = 512


def _vmem_limit(estimate_bytes):
    return int(min(estimate_bytes * 1.25 + 8 * MIB, V7X_VMEM_BYTES - 6 * MIB))


def _rms(x, g):
    ms = jnp.mean(x * x, axis=-1, keepdims=True)
    return x * lax.rsqrt(ms + NORM_EPS) * g


def _dot(a, b):
    return jnp.dot(a, b, preferred_element_type=F32)


def _dot_nt(a, b):
    return lax.dot_general(a, b, (((1,), (1,)), ((), ())), preferred_element_type=F32)


def _in_proj_kernel(h_ref, g_ref, w_ref, wdt_ref, dtb_ref, proj_ref, dt_ref, xn_ref):
    @pl.when(pl.program_id(1) == 0)
    def _():
        xn = _rms(h_ref[...], g_ref[...]).astype(BF16)
        xn_ref[...] = xn
        dt_ref[...] = jax.nn.softplus(_dot(xn, wdt_ref[...]) + dtb_ref[...])

    proj_ref[...] = _dot(xn_ref[...], w_ref[...]).astype(BF16)


def _in_proj(h, g, w_all, wdt_all, dtb_all, layer, *, tm=1024, tn=1408):
    t, d = h.shape
    n = w_all.shape[-1]
    assert t % tm == 0 and n % tn == 0
    est = 2 * tm * d * 4 + 2 * d * tn * 2 + 2 * tm * tn * 2 + tm * d * 2 + tm * tn * 4 + 2 * d * V7X_LANES * 2
    return pl.pallas_call(
        _in_proj_kernel,
        grid=(t // tm, n // tn),
        in_specs=[
            pl.BlockSpec((tm, d), lambda i, j: (i, 0)),
            pl.BlockSpec((None, 1, d), lambda i, j: (layer, 0, 0)),
            pl.BlockSpec((None, d, tn), lambda i, j: (layer, 0, j)),
            pl.BlockSpec((None, d, V7X_LANES), lambda i, j: (layer, 0, 0)),
            pl.BlockSpec((None, 1, V7X_LANES), lambda i, j: (layer, 0, 0)),
        ],
        out_specs=[
            pl.BlockSpec((tm, tn), lambda i, j: (i, j)),
            pl.BlockSpec((tm, V7X_LANES), lambda i, j: (i, 0)),
        ],
        out_shape=[jax.ShapeDtypeStruct((t, n), BF16), jax.ShapeDtypeStruct((t, V7X_LANES), F32)],
        scratch_shapes=[pltpu.VMEM((tm, d), BF16)],
        compiler_params=pltpu.CompilerParams(
            dimension_semantics=("parallel", "arbitrary"), vmem_limit_bytes=_vmem_limit(est)),
        name="in_proj",
    )(h, g, w_all, wdt_all, dtb_all)


def _split3(v):
    hi = v.astype(BF16)
    r1 = v - hi.astype(F32)
    mid = r1.astype(BF16)
    lo = (r1 - mid.astype(F32)).astype(BF16)
    return hi, mid, lo


def _conv_edge(tail, w_ref, cs, taps, r8):
    fix = None
    for j in range(1, taps):
        term = w_ref[taps - 1 - j:taps - j, cs] * jnp.where(r8 < j, pltpu.roll(tail, j, axis=0), 0.0)
        fix = term if fix is None else fix + term
    return fix


def _mixer_kernel(proj_ref, dt_ref, cw_ref, cb_ref, alog_ref, dsk_ref, gssd_ref, scw_ref, gsc_ref, e3_ref, shift_ref,
                  y_ref, xtail, xact, sctail, ybuf, state, *, d_ssd, d_sc):
    L = SSD_CHUNK
    pad = V7X_SUBLANES
    d_bc = SSD_GROUPS * SSD_STATE
    d_xbc = d_ssd + 2 * d_bc
    o_xbc = d_ssd
    o_u = o_xbc + d_xbc
    o_gb = o_u + d_sc
    o_gc = o_gb + d_sc
    heads_per_group = d_ssd // SSD_HEADDIM // SSD_GROUPS
    gcols = heads_per_group * SSD_HEADDIM
    pair = 2 * SSD_HEADDIM
    cw = MIXER_COLS

    @pl.when(pl.program_id(1) == 0)
    def _():
        xtail[...] = jnp.zeros_like(xtail)
        sctail[...] = jnp.zeros_like(sctail)
        state[...] = jnp.zeros_like(state)

    r8 = lax.broadcasted_iota(jnp.int32, (pad, cw), 0)

    for j0 in range(0, d_xbc, cw):
        cs = slice(j0, j0 + cw)
        x16 = proj_ref[0, :, o_xbc + j0:o_xbc + j0 + cw]
        shifted = _dot(shift_ref[...], x16)
        x = x16.astype(F32)
        acc = cb_ref[:, cs] + cw_ref[SSD_CONV - 1:SSD_CONV, cs] * x
        for j in range(1, SSD_CONV):
            acc = acc + cw_ref[SSD_CONV - 1 - j:SSD_CONV - j, cs] * shifted[(j - 1) * L:j * L]
        edge = acc[0:pad] + _conv_edge(xtail[:, cs], cw_ref, cs, SSD_CONV, r8)
        xact[:, cs] = jax.nn.silu(jnp.concatenate([edge, acc[pad:]], axis=0))
        xtail[:, cs] = x[L - pad:L]

    dt = dt_ref[0]
    adt = dt * (-jnp.exp(alog_ref[...]))
    row = lax.broadcasted_iota(jnp.int32, (L, L), 0)
    col = lax.broadcasted_iota(jnp.int32, (L, L), 1)
    tril = row >= col
    sums = _dot(tril.astype(BF16), jnp.concatenate(_split3(adt), axis=1))
    acs = sums[:, 0:V7X_LANES] + sums[:, V7X_LANES:2 * V7X_LANES] + sums[:, 2 * V7X_LANES:]
    acs2 = acs * LOG2_E
    acs2_t = acs2.T
    acs_last = acs[L - 1:L, :]
    per_head = jnp.concatenate(
        [dt, jnp.exp(acs), jnp.exp(acs_last - acs), jnp.broadcast_to(jnp.exp(acs_last), (2 * pad, V7X_LANES))], axis=0)
    per_head3 = jnp.concatenate(_split3(per_head), axis=1)
    lane = lax.broadcasted_iota(jnp.int32, (L, pair), 1)
    first_head = lane < SSD_HEADDIM

    for g in range(SSD_GROUPS):
        xs = slice(g * gcols, (g + 1) * gcols)
        bg = xact[:, d_ssd + g * SSD_STATE:d_ssd + (g + 1) * SSD_STATE]
        cg16 = xact[:, d_ssd + d_bc + g * SSD_STATE:d_ssd + d_bc + (g + 1) * SSD_STATE].astype(BF16)
        cb = _dot_nt(cg16, bg.astype(BF16))
        wide = _dot(per_head3, e3_ref[:, xs])
        xg = xact[:, xs]
        xdt = xg * wide[0:L]
        y_off = _dot(cg16, state[g].astype(BF16)) * wide[L:2 * L]
        for p in range(gcols // pair):
            ps = slice(p * pair, (p + 1) * pair)
            xp16 = xdt[:, ps].astype(BF16)
            parts = []
            for s in range(2):
                hd = g * heads_per_group + 2 * p + s
                seg2 = acs2[:, hd:hd + 1] - acs2_t[hd:hd + 1, :]
                decay = jnp.exp2(jnp.where(tril, seg2, -jnp.inf))
                parts.append(_dot((cb * decay).astype(BF16), xp16))
            yd = jnp.where(first_head, parts[0], parts[1])
            ys = slice(g * gcols + p * pair, g * gcols + (p + 1) * pair)
            y = yd + y_off[:, ps] + xg[:, ps] * dsk_ref[:, ys]
            ybuf[:, ys] = y * jax.nn.silu(proj_ref[0, :, ys].astype(F32))
        xrest = (xdt * wide[2 * L:3 * L]).astype(BF16)
        new = _dot(bg.T.astype(BF16), xrest)
        state[g] = state[g] * wide[3 * L:3 * L + 1] + new
        y_ref[0, :, xs] = _rms(ybuf[:, xs], gssd_ref[:, xs]).astype(BF16)

    gw = d_sc // SC_GROUPS
    for j0 in range(0, d_sc, cw):
        cs = slice(j0, j0 + cw)
        u = proj_ref[0, :, o_u + j0:o_u + j0 + cw].astype(F32)
        gate_c = proj_ref[0, :, o_gc + j0:o_gc + j0 + cw].astype(F32)
        gate_b = proj_ref[0, :, o_gb + j0:o_gb + j0 + cw].astype(F32)
        cu = gate_c * u
        acc = scw_ref[SC_CONV - 1:SC_CONV, cs] * cu
        for j in range(1, SC_CONV):
            acc = acc + scw_ref[SC_CONV - 1 - j:SC_CONV - j, cs] * pltpu.roll(cu, j, axis=0)
        head = cu[0:pad]
        edge = scw_ref[SC_CONV - 1:SC_CONV, cs] * head
        tail = sctail[:, cs]
        for j in range(1, SC_CONV):
            prev = jnp.where(r8 < j, pltpu.roll(tail, j, axis=0), pltpu.roll(head, j, axis=0))
            edge = edge + scw_ref[SC_CONV - 1 - j:SC_CONV - j, cs] * prev
        v = gate_b * jnp.concatenate([edge, acc[pad:]], axis=0)
        sctail[:, cs] = cu[L - pad:L]
        for q0 in range(0, cw, gw):
            y_ref[0, :, d_ssd + j0 + q0:d_ssd + j0 + q0 + gw] = _rms(
                v[:, q0:q0 + gw], gsc_ref[:, j0 + q0:j0 + q0 + gw]).astype(BF16)


def _mixer(proj, dt, cw, cb, alog, dsk, gssd, scw, gsc, e3, shift, layer, *, d_ssd, d_sc):
    b, s, n = proj.shape
    L = SSD_CHUNK
    d_xbc = d_ssd + 2 * SSD_GROUPS * SSD_STATE
    gcols = d_ssd // SSD_GROUPS
    assert s % L == 0 and gcols % (2 * SSD_HEADDIM) == 0 and (d_sc // SC_GROUPS) % V7X_LANES == 0
    assert d_xbc % MIXER_COLS == 0 and d_sc % MIXER_COLS == 0 and MIXER_COLS % (d_sc // SC_GROUPS) == 0
    pad = V7X_SUBLANES
    est = (2 * L * n * 2 + 2 * L * (d_ssd + d_sc) * 2 + 2 * 3 * V7X_LANES * d_ssd * 2 + 2 * 3 * L * L * 2
           + pad * (d_xbc + d_sc) * 4 + L * d_xbc * 4 + L * d_ssd * 4 + SSD_STATE * d_ssd * 4
           + 8 * L * gcols * 4)
    lay = lambda shape: pl.BlockSpec((None,) + shape, lambda bi, ci: (layer, 0, 0))
    return pl.pallas_call(
        functools.partial(_mixer_kernel, d_ssd=d_ssd, d_sc=d_sc),
        grid=(b, s // L),
        in_specs=[
            pl.BlockSpec((1, L, n), lambda bi, ci: (bi, ci, 0)),
            pl.BlockSpec((1, L, V7X_LANES), lambda bi, ci: (bi, ci, 0)),
            lay((SSD_CONV, d_xbc)), lay((1, d_xbc)), lay((1, V7X_LANES)), lay((1, d_ssd)), lay((1, d_ssd)),
            lay((SC_CONV, d_sc)), lay((1, d_sc)),
            pl.BlockSpec(e3.shape, lambda bi, ci: (0, 0)),
            pl.BlockSpec(shift.shape, lambda bi, ci: (0, 0)),
        ],
        out_specs=pl.BlockSpec((1, L, d_ssd + d_sc), lambda bi, ci: (bi, ci, 0)),
        out_shape=jax.ShapeDtypeStruct((b, s, d_ssd + d_sc), BF16),
        scratch_shapes=[
            pltpu.VMEM((pad, d_xbc), F32),
            pltpu.VMEM((L, d_xbc), F32),
            pltpu.VMEM((pad, d_sc), F32),
            pltpu.VMEM((L, d_ssd), F32),
            pltpu.VMEM((SSD_GROUPS, SSD_STATE, gcols), F32),
        ],
        compiler_params=pltpu.CompilerParams(
            dimension_semantics=("parallel", "arbitrary"), vmem_limit_bytes=_vmem_limit(est)),
        name="mixer",
    )(proj, dt, cw, cb, alog, dsk, gssd, scw, gsc, e3, shift)


def _out_proj_kernel(y_ref, w_ref, h_ref, o_ref):
    o_ref[...] = h_ref[...] + _dot(y_ref[...], w_ref[...])


def _out_proj(y, w_all, h, layer, *, tm=512):
    t, k = y.shape
    d = h.shape[-1]
    est = 2 * tm * k * 2 + k * d * 2 + 4 * tm * d * 4 + tm * d * 4
    return pl.pallas_call(
        _out_proj_kernel,
        grid=(t // tm,),
        in_specs=[
            pl.BlockSpec((tm, k), lambda i: (i, 0)),
            pl.BlockSpec((None, k, d), lambda i: (layer, 0, 0), pipeline_mode=pl.Buffered(1)),
            pl.BlockSpec((tm, d), lambda i: (i, 0)),
        ],
        out_specs=pl.BlockSpec((tm, d), lambda i: (i, 0)),
        out_shape=jax.ShapeDtypeStruct((t, d), F32),
        compiler_params=pltpu.CompilerParams(
            dimension_semantics=("parallel",), vmem_limit_bytes=_vmem_limit(est)),
        name="out_proj",
    )(y, w_all, h)


def _xattn_kernel(h_ref, g_ref, wq_ref, k_ref, v_ref, wo_ref, o_ref):
    h = h_ref[...]
    xn = _rms(h, g_ref[...]).astype(BF16)
    q = _dot(xn, wq_ref[...]).astype(BF16)
    scale = XA_HEADDIM ** -0.5
    heads = []
    for hd in range(XA_HEADS):
        cs = slice(hd * XA_HEADDIM, (hd + 1) * XA_HEADDIM)
        s = _dot_nt(q[:, cs], k_ref[:, cs]) * scale
        p = jnp.exp(s - jnp.max(s, axis=-1, keepdims=True))
        p = p / jnp.sum(p, axis=-1, keepdims=True)
        heads.append(_dot(p.astype(BF16), v_ref[:, cs]).astype(BF16))
    o_ref[...] = h + _dot(jnp.concatenate(heads, axis=-1), wo_ref[...])


def _xattn(h, g, wq_all, kv, wo_all, layer, depth, *, n_mem, seq, tm=512):
    t, d = h.shape
    d_xa = XA_HEADS * XA_HEADDIM
    assert seq % tm == 0
    per_batch = seq // tm
    est = 4 * tm * d * 4 + 2 * 2 * d * d_xa * 2 + 4 * n_mem * d_xa * 2 + tm * d * 6 + 4 * tm * n_mem * 4
    return pl.pallas_call(
        _xattn_kernel,
        grid=(t // tm,),
        in_specs=[
            pl.BlockSpec((tm, d), lambda i: (i, 0)),
            pl.BlockSpec((None, 1, d), lambda i: (layer, 0, 0)),
            pl.BlockSpec((None, d, d_xa), lambda i: (layer, 0, 0)),
            pl.BlockSpec((n_mem, d_xa), lambda i: (i // per_batch, layer)),
            pl.BlockSpec((n_mem, d_xa), lambda i: (i // per_batch, depth + layer)),
            pl.BlockSpec((None, d_xa, d), lambda i: (layer, 0, 0)),
        ],
        out_specs=pl.BlockSpec((tm, d), lambda i: (i, 0)),
        out_shape=jax.ShapeDtypeStruct((t, d), F32),
        compiler_params=pltpu.CompilerParams(
            dimension_semantics=("parallel",), vmem_limit_bytes=_vmem_limit(est)),
        name="xattn",
    )(h, g, wq_all, kv, kv, wo_all)


def _kv_kernel(mem_ref, g_ref, w_ref, o_ref):
    o_ref[...] = _dot(_rms(mem_ref[...], g_ref[...]).astype(BF16), w_ref[...]).astype(BF16)


def _kv_proj(mem, g, wkv, *, tn=512):
    m, d = mem.shape
    n = wkv.shape[-1]
    est = 2 * m * d * 4 + 2 * d * tn * 2 + 2 * m * tn * 2 + m * d * 6
    return pl.pallas_call(
        _kv_kernel,
        grid=(n // tn,),
        in_specs=[
            pl.BlockSpec((m, d), lambda j: (0, 0)),
            pl.BlockSpec((1, d), lambda j: (0, 0)),
            pl.BlockSpec((d, tn), lambda j: (0, j)),
        ],
        out_specs=pl.BlockSpec((m, tn), lambda j: (0, j)),
        out_shape=jax.ShapeDtypeStruct((m, n), BF16),
        compiler_params=pltpu.CompilerParams(
            dimension_semantics=("parallel",), vmem_limit_bytes=_vmem_limit(est)),
        name="kv_proj",
    )(mem, g, wkv)


def _ffn_kernel(*refs, out_norm):
    if out_norm:
        h_ref, g_ref, wg_ref, wu_ref, wd_ref, gout_ref, o_ref, xn_ref = refs
    else:
        h_ref, g_ref, wg_ref, wu_ref, wd_ref, o_ref, xn_ref = refs

    @pl.when(pl.program_id(1) == 0)
    def _():
        h = h_ref[...]
        xn_ref[...] = _rms(h, g_ref[...]).astype(BF16)
        o_ref[...] = h

    xn = xn_ref[...]
    act = (jax.nn.silu(_dot(xn, wg_ref[...])) * _dot(xn, wu_ref[...])).astype(BF16)
    o_ref[...] += _dot(act, wd_ref[...])

    if out_norm:
        @pl.when(pl.program_id(1) == pl.num_programs(1) - 1)
        def _():
            o_ref[...] = _rms(o_ref[...], gout_ref[...])


def _ffn(h, g, wg_all, wu_all, wd_all, layer, g_out=None, *, tm=512, tf=512):
    t, d = h.shape
    f = wg_all.shape[-1]
    est = 4 * tm * d * 4 + tm * d * 2 + 3 * 2 * d * tf * 2 + 3 * tm * tf * 4 + tm * d * 4
    in_specs = [
        pl.BlockSpec((tm, d), lambda i, j: (i, 0)),
        pl.BlockSpec((None, 1, d), lambda i, j: (layer, 0, 0)),
        pl.BlockSpec((None, d, tf), lambda i, j: (layer, 0, j)),
        pl.BlockSpec((None, d, tf), lambda i, j: (layer, 0, j)),
        pl.BlockSpec((None, tf, d), lambda i, j: (layer, j, 0)),
    ]
    args = [h, g, wg_all, wu_all, wd_all]
    if g_out is not None:
        in_specs.append(pl.BlockSpec((1, d), lambda i, j: (0, 0)))
        args.append(g_out)
    return pl.pallas_call(
        functools.partial(_ffn_kernel, out_norm=g_out is not None),
        grid=(t // tm, f // tf),
        in_specs=in_specs,
        out_specs=pl.BlockSpec((tm, d), lambda i, j: (i, 0)),
        out_shape=jax.ShapeDtypeStruct((t, d), F32),
        scratch_shapes=[pltpu.VMEM((tm, d), BF16)],
        compiler_params=pltpu.CompilerParams(
            dimension_semantics=("parallel", "arbitrary"), vmem_limit_bytes=_vmem_limit(est)),
        name="ffn",
    )(*args)


def kernel(x, mem, norm_mix, w_in, ssd_conv_w, ssd_conv_b, dt_bias, a_log, d_skip, ssd_norm, sc_conv_w, sc_norm,
           w_out, mem_norm, norm_xa, w_q, w_k, w_v, w_o, norm_ffn, w_gate, w_up, w_down, norm_final):
    b, s, d = x.shape
    depth = w_in.shape[0]
    n_mem = mem.shape[1]
    d_ssd = ssd_norm.shape[-1]
    d_sc = sc_norm.shape[-1]
    n_heads = dt_bias.shape[-1]
    d_xbc = ssd_conv_w.shape[-1]
    assert n_heads * SSD_HEADDIM == d_ssd and n_heads <= V7X_LANES

    c_dt = d_ssd + d_xbc
    w_main = jnp.concatenate([w_in[:, :, :c_dt], w_in[:, :, c_dt + n_heads:]], axis=-1).astype(BF16)
    lane_pad = ((0, 0), (0, 0), (0, V7X_LANES - n_heads))
    w_dt = jnp.pad(w_in[:, :, c_dt:c_dt + n_heads], lane_pad).astype(BF16)
    dtb = jnp.pad(dt_bias[:, None, :], lane_pad)
    alog = jnp.pad(a_log[:, None, :], lane_pad)
    dsk = jnp.repeat(d_skip, SSD_HEADDIM, axis=-1)[:, None, :]
    head_of_col = jnp.arange(d_ssd, dtype=jnp.int32)[None, :] // SSD_HEADDIM
    expand = head_of_col == jnp.arange(V7X_LANES, dtype=jnp.int32)[:, None]
    expand3 = jnp.tile(expand, (3, 1)).astype(BF16)
    t_idx = jnp.arange(SSD_CHUNK, dtype=jnp.int32)
    shift = jnp.concatenate([t_idx[:, None] - j == t_idx[None, :] for j in range(1, SSD_CONV)], axis=0).astype(BF16)
    w_out16, w_q16, w_o16 = w_out.astype(BF16), w_q.astype(BF16), w_o.astype(BF16)
    w_gate16, w_up16, w_down16 = w_gate.astype(BF16), w_up.astype(BF16), w_down.astype(BF16)
    w_kv = jnp.concatenate([w_k[i] for i in range(depth)] + [w_v[i] for i in range(depth)], axis=-1).astype(BF16)
    row = lambda p: p[:, None, :]

    kv = _kv_proj(mem.reshape(b * n_mem, d), mem_norm[None, :], w_kv)
    h = x.reshape(b * s, d)
    for i in range(depth):
        proj, dt = _in_proj(h, row(norm_mix), w_main, w_dt, dtb, i)
        y = _mixer(proj.reshape(b, s, -1), dt.reshape(b, s, V7X_LANES), ssd_conv_w, row(ssd_conv_b), alog, dsk,
                   row(ssd_norm), sc_conv_w, row(sc_norm), expand3, shift, i, d_ssd=d_ssd, d_sc=d_sc)
        h = _out_proj(y.reshape(b * s, d_ssd + d_sc), w_out16, h, i)
        h = _xattn(h, row(norm_xa), w_q16, kv, w_o16, i, depth, n_mem=n_mem, seq=s)
        g_out = norm_final[None, :] if i == depth - 1 else None
        h = _ffn(h, row(norm_ffn), w_gate16, w_up16, w_down16, i, g_out)
    return h.reshape(b, s, d)
```
